```python
import jax, jax.numpy as jnp
from jax import lax
import numpy as np

D_MODEL = 2048
BATCH = 2
SEQ = 8192
DEPTH = 2
DEC_BATCH = 32
DEC_SEQ = 32
PAST_LEN = 1024

CHUNK = 64
Q_BLOCK = 128
N_HEADS = 8
Q_LORA = 512
KV_LORA = 512
NOPE_DIM = 128
ROPE_DIM = 64
V_DIM = 128
ROPE_THETA = 10000.0
ATTN_WIDTH = N_HEADS * V_DIM
ATTN_SCALE = (NOPE_DIM + ROPE_DIM) ** -0.5
CONV_CH = 1024
CONV_W = 31
MIX_WIDTH = ATTN_WIDTH + CONV_CH
IN_COLS = Q_LORA + KV_LORA + ROPE_DIM + 2 * CONV_CH
N_GROUPS = 8
EXPERTS_PER_GROUP = 8
N_EXPERTS = N_GROUPS * EXPERTS_PER_GROUP
TOP_K = 2
D_EXPERT = 512
EXPERT_BLOCK = 128
RMS_EPS = 1e-6
LN_EPS = 1e-5

kernel_name = "mla_conformer_conv_hier_moe_stream"


def _rmsnorm(x, g):
    xf = x.astype(jnp.float32)
    y = xf * lax.rsqrt(jnp.mean(xf * xf, axis=-1, keepdims=True) + RMS_EPS)
    return (y * g.astype(jnp.float32)).astype(x.dtype)


def _layernorm(x, g, b):
    xf = x.astype(jnp.float32)
    mu = jnp.mean(xf, axis=-1, keepdims=True)
    xc = xf - mu
    var = jnp.mean(xc * xc, axis=-1, keepdims=True)
    return (xc * lax.rsqrt(var + LN_EPS) * g.astype(jnp.float32) + b.astype(jnp.float32)).astype(x.dtype)


def _rope_tables(pos):
    half = ROPE_DIM // 2
    inv = ROPE_THETA ** (-jnp.arange(half, dtype=jnp.float32) / half)
    ang = pos.astype(jnp.float32)[:, None] * inv[None, :]
    return jnp.cos(ang), jnp.sin(ang)


def _rope(x, cos, sin):
    xf = x.astype(jnp.float32)
    x1, x2 = xf[..., : ROPE_DIM // 2], xf[..., ROPE_DIM // 2:]
    return jnp.concatenate([x1 * cos - x2 * sin, x2 * cos + x1 * sin], axis=-1).astype(x.dtype)


def _attend(q_nope, q_rope, k_nope, k_rope, v, q_pos, k_pos):
    s = (jnp.einsum('bqhd,bkhd->bhqk', q_nope, k_nope)
         + jnp.einsum('bqhd,bkd->bhqk', q_rope, k_rope)).astype(jnp.float32) * ATTN_SCALE
    mask = (k_pos[None, :] // CHUNK) <= (q_pos[:, None] // CHUNK)
    s = jnp.where(mask[None, None], s, -jnp.inf)
    p = jax.nn.softmax(s, axis=-1).astype(v.dtype)
    return jnp.einsum('bhqk,bkhd->bqhd', p, v)


def _mixer(h, past_kv, past_kr, past_cv, w_in, q_norm_g, w_uq, kv_norm_g, w_ukv,
           conv_w, conv_b, conv_ln_g, conv_ln_b, w_out):
    B, S, _ = h.shape
    P = past_kv.shape[1]
    z = h @ w_in
    q_lat, kv_lat, k_rope, conv_in = jnp.split(
        z, [Q_LORA, Q_LORA + KV_LORA, Q_LORA + KV_LORA + ROPE_DIM], axis=-1)
    q = (_rmsnorm(q_lat, q_norm_g) @ w_uq).reshape(B, S, N_HEADS, NOPE_DIM + ROPE_DIM)
    q_nope, q_rope = q[..., :NOPE_DIM], q[..., NOPE_DIM:]
    q_pos = P + jnp.arange(S)
    cos, sin = _rope_tables(q_pos)
    q_rope = _rope(q_rope, cos[:, None, :], sin[:, None, :])
    c_kv = _rmsnorm(kv_lat, kv_norm_g)
    k_rope = _rope(k_rope, cos, sin)
    c_all = jnp.concatenate([past_kv, c_kv], axis=1)
    kr_all = jnp.concatenate([past_kr, k_rope], axis=1)
    kv = (c_all @ w_ukv).reshape(B, P + S, N_HEADS, NOPE_DIM + V_DIM)
    k_nope, v = kv[..., :NOPE_DIM], kv[..., NOPE_DIM:]
    k_pos = jnp.arange(P + S)
    if S > Q_BLOCK and S % Q_BLOCK == 0:
        nb = S // Q_BLOCK
        qn = q_nope.reshape(B, nb, Q_BLOCK, N_HEADS, NOPE_DIM).transpose(1, 0, 2, 3, 4)
        qr = q_rope.reshape(B, nb, Q_BLOCK, N_HEADS, ROPE_DIM).transpose(1, 0, 2, 3, 4)
        qp = q_pos.reshape(nb, Q_BLOCK)
        o = lax.map(lambda a: _attend(a[0], a[1], k_nope, kr_all, v, a[2], k_pos), (qn, qr, qp))
        attn = o.transpose(1, 0, 2, 3, 4).reshape(B, S, ATTN_WIDTH)
    else:
        attn = _attend(q_nope, q_rope, k_nope, kr_all, v, q_pos, k_pos).reshape(B, S, ATTN_WIDTH)
    a, gte = jnp.split(conv_in, 2, axis=-1)
    u = a * jax.nn.sigmoid(gte)
    u_all = jnp.concatenate([past_cv, u], axis=1)
    dw = lax.conv_general_dilated(u_all, conv_w[:, None, :], (1,), 'VALID',
                                  dimension_numbers=('NWC', 'WIO', 'NWC'),
                                  feature_group_count=CONV_CH) + conv_b
    cv = jax.nn.silu(_layernorm(dw, conv_ln_g, conv_ln_b))
    out = jnp.concatenate([attn, cv], axis=-1) @ w_out
    return out, c_kv, k_rope, u_all[:, -(CONV_W - 1):]


def _dispatch(t, eidx, ew, w_gate, w_up, w_down):
    T, D = t.shape
    A = T * TOP_K
    flat_e = eidx.reshape(-1)
    flat_w = ew.reshape(-1)
    flat_tok = jnp.repeat(jnp.arange(T, dtype=jnp.int32), TOP_K)
    order = jnp.argsort(flat_e)
    se = flat_e[order]
    counts = jnp.bincount(flat_e, length=N_EXPERTS)
    start = jnp.cumsum(counts) - counts
    padded = ((counts + EXPERT_BLOCK - 1) // EXPERT_BLOCK) * EXPERT_BLOCK
    pend = jnp.cumsum(padded)
    pstart = pend - padded
    dest = pstart[se] + jnp.arange(A) - start[se]
    n_blocks = -(-A // EXPERT_BLOCK) + N_EXPERTS
    P = n_blocks * EXPERT_BLOCK
    slot_tok = jnp.full((P,), T, jnp.int32).at[dest].set(flat_tok[order])
    slot_w = jnp.zeros((P,), t.dtype).at[dest].set(flat_w[order])
    block_e = jnp.minimum(jnp.searchsorted(pend, jnp.arange(n_blocks) * EXPERT_BLOCK, side='right'),
                          N_EXPERTS - 1)
    t_pad = jnp.concatenate([t, jnp.zeros((1, D), t.dtype)], axis=0)
    xb = t_pad[slot_tok].reshape(n_blocks, EXPERT_BLOCK, D)

    def expert_block(a):
        xblk, e = a
        return (jax.nn.silu(xblk @ w_gate[e]) * (xblk @ w_up[e])) @ w_down[e]

    yb = lax.map(expert_block, (xb, block_e)).reshape(P, D)
    y = jnp.zeros((T + 1, D), t.dtype).at[slot_tok].add(yb * slot_w[:, None])
    return y[:T]


def _hier_moe(h, w_rg, b_rg, w_re, b_re, w_gate, w_up, w_down):
    B, S, D = h.shape
    t = h.reshape(B * S, D)
    T = B * S
    pg = jax.nn.softmax((t @ w_rg).astype(jnp.float32) + b_rg.astype(jnp.float32), axis=-1)
    gv, gi = lax.top_k(pg, 1)
    g = gi[:, 0]
    le = ((t @ w_re).astype(jnp.float32) + b_re.astype(jnp.float32)).reshape(T, N_GROUPS, EXPERTS_PER_GROUP)
    pe = jax.nn.softmax(le[jnp.arange(T), g], axis=-1)
    ev, ei = lax.top_k(pe, TOP_K)
    w = ev / jnp.sum(ev, axis=-1, keepdims=True) * gv
    eidx = g[:, None] * EXPERTS_PER_GROUP + ei
    return _dispatch(t, eidx, w.astype(t.dtype), w_gate, w_up, w_down).reshape(B, S, D)


def _layer(x, past_kv, past_kr, past_cv, norm1_g, w_in, q_norm_g, w_uq, kv_norm_g, w_ukv,
           conv_w, conv_b, conv_ln_g, conv_ln_b, w_out, norm2_g, w_rg, b_rg, w_re, b_re,
           w_gate, w_up, w_down):
    o, c_kv, k_rope, cv_state = _mixer(_rmsnorm(x, norm1_g), past_kv, past_kr, past_cv, w_in,
                                       q_norm_g, w_uq, kv_norm_g, w_ukv, conv_w, conv_b,
                                       conv_ln_g, conv_ln_b, w_out)
    x = x + o
    x = x + _hier_moe(_rmsnorm(x, norm2_g), w_rg, b_rg, w_re, b_re, w_gate, w_up, w_down)
    return x, c_kv, k_rope, cv_state


def setup_inputs(seed: int = 0) -> dict:
    key = jax.random.key(seed)
    ks = jax.random.split(key, 32)
    f32 = jnp.float32
    nrm = lambda k, shape, s: jax.random.normal(k, shape, f32) * s
    return {
        "x_prompt": nrm(ks[0], (BATCH, SEQ, D_MODEL), 1.0),
        "x_sample": nrm(ks[1], (DEC_BATCH, DEC_SEQ, D_MODEL), 1.0),
        "cache_kv": nrm(ks[2], (DEPTH, DEC_BATCH, PAST_LEN, KV_LORA), 1.0),
        "cache_krope": nrm(ks[3], (DEPTH, DEC_BATCH, PAST_LEN, ROPE_DIM), 1.0),
        "state_conv": nrm(ks[4], (DEPTH, DEC_BATCH, CONV_W - 1, CONV_CH), 0.5),
        "norm1_g": 1.0 + nrm(ks[5], (DEPTH, D_MODEL), 0.02),
        "w_in": nrm(ks[6], (DEPTH, D_MODEL, IN_COLS), D_MODEL ** -0.5),
        "q_norm_g": 1.0 + nrm(ks[7], (DEPTH, Q_LORA), 0.02),
        "w_uq": nrm(ks[8], (DEPTH, Q_LORA, N_HEADS * (NOPE_DIM + ROPE_DIM)), Q_LORA ** -0.5),
        "kv_norm_g": 1.0 + nrm(ks[9], (DEPTH, KV_LORA), 0.02),
        "w_ukv": nrm(ks[10], (DEPTH, KV_LORA, N_HEADS * (NOPE_DIM + V_DIM)), KV_LORA ** -0.5),
        "conv_w": nrm(ks[11], (DEPTH, CONV_W, CONV_CH), CONV_W ** -0.5),
        "conv_b": nrm(ks[12], (DEPTH, CONV_CH), 0.02),
        "conv_ln_g": 1.0 + nrm(ks[13], (DEPTH, CONV_CH), 0.02),
        "conv_ln_b": nrm(ks[14], (DEPTH, CONV_CH), 0.02),
        "w_out": nrm(ks[15], (DEPTH, MIX_WIDTH, D_MODEL), MIX_WIDTH ** -0.5),
        "norm2_g": 1.0 + nrm(ks[16], (DEPTH, D_MODEL), 0.02),
        "w_router_group": nrm(ks[17], (DEPTH, D_MODEL, N_GROUPS), D_MODEL ** -0.5),
        "b_router_group": nrm(ks[18], (DEPTH, N_GROUPS), 0.01),
        "w_router_expert": nrm(ks[19], (DEPTH, D_MODEL, N_EXPERTS), D_MODEL ** -0.5),
        "b_router_expert": nrm(ks[20], (DEPTH, N_EXPERTS), 0.01),
        "w_gate": nrm(ks[21], (DEPTH, N_EXPERTS, D_MODEL, D_EXPERT), D_MODEL ** -0.5),
        "w_up": nrm(ks[22], (DEPTH, N_EXPERTS, D_MODEL, D_EXPERT), D_MODEL ** -0.5),
        "w_down": nrm(ks[23], (DEPTH, N_EXPERTS, D_EXPERT, D_MODEL), D_EXPERT ** -0.5),
        "final_norm_g": 1.0 + nrm(ks[24], (D_MODEL,), 0.02),
    }


def reference(x_prompt, x_sample, cache_kv, cache_krope, state_conv, norm1_g, w_in, q_norm_g,
              w_uq, kv_norm_g, w_ukv, conv_w, conv_b, conv_ln_g, conv_ln_b, w_out, norm2_g,
              w_router_group, b_router_group, w_router_expert, b_router_expert, w_gate, w_up,
              w_down, final_norm_g):
    B = x_prompt.shape[0]
    dt = x_prompt.dtype
    xp, xs = x_prompt, x_sample
    kv_p, kr_p, cv_p, kv_s, kr_s, cv_s = [], [], [], [], [], []
    for l in range(DEPTH):
        wl = (norm1_g[l], w_in[l], q_norm_g[l], w_uq[l], kv_norm_g[l], w_ukv[l], conv_w[l],
              conv_b[l], conv_ln_g[l], conv_ln_b[l], w_out[l], norm2_g[l], w_router_group[l],
              b_router_group[l], w_router_expert[l], b_router_expert[l], w_gate[l], w_up[l],
              w_down[l])
        xp, a, b, c = _layer(xp, jnp.zeros((B, 0, KV_LORA), dt), jnp.zeros((B, 0, ROPE_DIM), dt),
                             jnp.zeros((B, CONV_W - 1, CONV_CH), dt), *wl)
        kv_p.append(a); kr_p.append(b); cv_p.append(c)
        xs, a, b, c = _layer(xs, cache_kv[l], cache_krope[l], state_conv[l], *wl)
        kv_s.append(a); kr_s.append(b); cv_s.append(c)
    y_prompt = _rmsnorm(xp, final_norm_g)
    y_sample = _rmsnorm(xs, final_norm_g)
    return (y_prompt, y_sample, jnp.stack(kv_p), jnp.stack(kr_p), jnp.stack(cv_p),
            jnp.stack(kv_s), jnp.stack(kr_s), jnp.stack(cv_s))
```

```python
import functools

import numpy as np
import jax
import jax.numpy as jnp
from jax import lax
from jax.experimental import pallas as pl
from jax.experimental.pallas import tpu as pltpu

N_HEADS = 8
NOPE_DIM = 128
ROPE_DIM = 64
V_DIM = 128
ROPE_THETA = 10000.0
ATTN_SCALE = (NOPE_DIM + ROPE_DIM) ** -0.5
CHUNK = 64
N_GROUPS = 8
EXPERTS_PER_GROUP = 8
N_EXPERTS = N_GROUPS * EXPERTS_PER_GROUP
TOP_K = 2
RMS_EPS = 1e-6
LN_EPS = 1e-5

LANES = 128
SUBLANES = 8
VMEM_LIMIT_BYTES = 56 * 1024 * 1024

HEAD_W = 2 * LANES
ROPE_PAD = LANES
NEG_BIG = -1e30

BF16 = jnp.bfloat16
F32 = jnp.float32


def _cparams(semantics):
    return pltpu.CompilerParams(dimension_semantics=semantics, vmem_limit_bytes=VMEM_LIMIT_BYTES)


def _const_spec(shape):
    return pl.BlockSpec(shape, lambda *_: (0,) * len(shape), pipeline_mode=pl.Buffered(1))


def _rms(x, g):
    return x * lax.rsqrt(jnp.mean(x * x, axis=-1, keepdims=True) + RMS_EPS) * g


def _sigmoid(x):
    return 1.0 / (1.0 + jnp.exp(-x))


def _row_tile(n, want):
    t = min(want, n)
    while n % t:
        t //= 2
    return t


def _inproj_kernel(x_ref, g1_ref, win_ref, qg_ref, wuq_ref, kvg_ref, wukv_ref, cos_ref, sin_ref,
                   q_ref, ckv_ref, kr_ref, k_ref, v_ref, u_ref, *, q_lora, kv_lora, conv_ch):
    h = _rms(x_ref[...], g1_ref[...])
    z = jnp.dot(h.astype(BF16), win_ref[...], preferred_element_type=F32)
    c0 = q_lora
    c1 = c0 + kv_lora
    c2 = c1 + conv_ch
    c3 = c2 + conv_ch
    c4 = c3 + ROPE_PAD
    cos = cos_ref[...]
    sin = sin_ref[...]
    u_ref[...] = z[:, c1:c2] * _sigmoid(z[:, c2:c3])
    qn = _rms(z[:, :c0], qg_ref[...])
    qq = jnp.dot(qn.astype(BF16), wuq_ref[...], preferred_element_type=F32)
    rot0 = N_HEADS * HEAD_W
    for hd in range(N_HEADS):
        b = hd * HEAD_W
        q_ref[:, b:b + LANES] = (qq[:, b:b + LANES] * ATTN_SCALE).astype(BF16)
        rope = (qq[:, b + LANES:b + HEAD_W] * cos
                + qq[:, rot0 + hd * ROPE_PAD:rot0 + (hd + 1) * ROPE_PAD] * sin)
        q_ref[:, b + LANES:b + HEAD_W] = (rope * ATTN_SCALE).astype(BF16)
    ckv = _rms(z[:, c0:c1], kvg_ref[...])
    ckv_ref[...] = ckv
    kvx = jnp.dot(ckv.astype(BF16), wukv_ref[...], preferred_element_type=F32)
    kr = z[:, c3:c4] * cos + z[:, c4:c4 + ROPE_PAD] * sin
    kr_ref[...] = kr[:, :ROPE_DIM]
    krb = kr.astype(BF16)
    for hd in range(N_HEADS):
        b = hd * HEAD_W
        k_ref[:, b:b + LANES] = kvx[:, hd * NOPE_DIM:(hd + 1) * NOPE_DIM].astype(BF16)
        k_ref[:, b + LANES:b + HEAD_W] = krb
    v_ref[...] = kvx[:, N_HEADS * NOPE_DIM:].astype(BF16)


def _inproj(x, g1, win, qg, wuq, kvg, wukv, cos, sin, *, conv_ch):
    t, d = x.shape
    q_lora = qg.shape[1]
    kv_lora = kvg.shape[1]
    tm = _row_tile(t, 256)
    row = lambda w: pl.BlockSpec((tm, w), lambda i: (i, 0))
    kern = functools.partial(_inproj_kernel, q_lora=q_lora, kv_lora=kv_lora, conv_ch=conv_ch)
    return pl.pallas_call(
        kern,
        grid=(t // tm,),
        in_specs=[row(d), _const_spec(g1.shape), _const_spec(win.shape), _const_spec(qg.shape),
                  _const_spec(wuq.shape), _const_spec(kvg.shape), _const_spec(wukv.shape),
                  row(ROPE_PAD), row(ROPE_PAD)],
        out_specs=[row(N_HEADS * HEAD_W), row(kv_lora), row(ROPE_DIM), row(N_HEADS * HEAD_W),
                   row(N_HEADS * V_DIM), row(conv_ch)],
        out_shape=[jax.ShapeDtypeStruct((t, N_HEADS * HEAD_W), BF16),
                   jax.ShapeDtypeStruct((t, kv_lora), F32),
                   jax.ShapeDtypeStruct((t, ROPE_DIM), F32),
                   jax.ShapeDtypeStruct((t, N_HEADS * HEAD_W), BF16),
                   jax.ShapeDtypeStruct((t, N_HEADS * V_DIM), BF16),
                   jax.ShapeDtypeStruct((t, conv_ch), F32)],
        compiler_params=_cparams(("parallel",)),
        name="inproj",
    )(x, g1, win, qg, wuq, kvg, wukv, cos, sin)


def _expand_kernel(ckv_ref, kr_ref, wukv_ref, k_ref, v_ref):
    kvx = jnp.dot(ckv_ref[...].astype(BF16), wukv_ref[...], preferred_element_type=F32)
    krb = kr_ref[...].astype(BF16)
    pad = jnp.zeros((krb.shape[0], HEAD_W - LANES - ROPE_DIM), BF16)
    for hd in range(N_HEADS):
        b = hd * HEAD_W
        k_ref[:, b:b + LANES] = kvx[:, hd * NOPE_DIM:(hd + 1) * NOPE_DIM].astype(BF16)
        k_ref[:, b + LANES:b + LANES + ROPE_DIM] = krb
        k_ref[:, b + LANES + ROPE_DIM:b + HEAD_W] = pad
    v_ref[...] = kvx[:, N_HEADS * NOPE_DIM:].astype(BF16)


def _expand_cache(ckv, kr, wukv):
    n, kv_lora = ckv.shape
    tm = _row_tile(n, 512)
    row = lambda w: pl.BlockSpec((tm, w), lambda i: (i, 0))
    return pl.pallas_call(
        _expand_kernel,
        grid=(n // tm,),
        in_specs=[row(kv_lora), row(ROPE_DIM), _const_spec(wukv.shape)],
        out_specs=[row(N_HEADS * HEAD_W), row(N_HEADS * V_DIM)],
        out_shape=[jax.ShapeDtypeStruct((n, N_HEADS * HEAD_W), BF16),
                   jax.ShapeDtypeStruct((n, N_HEADS * V_DIM), BF16)],
        compiler_params=_cparams(("parallel",)),
        name="expand_cache",
    )(ckv, kr, wukv)


def _flash_kernel(qi_ref, kj_ref, q_ref, k_ref, v_ref, o_ref, m_scr, l_scr, acc_scr, bias_scr):
    p = pl.program_id(1)
    qi = qi_ref[p]
    kj = kj_ref[p]
    tq = q_ref.shape[1]
    tk = k_ref.shape[1]

    @pl.when(kj == 0)
    def _():
        m_scr[...] = jnp.full(m_scr.shape, NEG_BIG, F32)
        l_scr[...] = jnp.zeros(l_scr.shape, F32)
        acc_scr[...] = jnp.zeros(acc_scr.shape, F32)

    shift = CHUNK.bit_length() - 1
    qc = (lax.broadcasted_iota(jnp.int32, (tq, tk), 0) + qi * tq) >> shift
    kc = (lax.broadcasted_iota(jnp.int32, (tq, tk), 1) + kj * tk) >> shift
    bias_scr[...] = jnp.where(kc <= qc, 0.0, NEG_BIG).astype(F32)

    reps = tk // LANES
    for hd in range(N_HEADS):
        qh = q_ref[0, :, hd * HEAD_W:(hd + 1) * HEAD_W]
        kh = k_ref[0, :, hd * HEAD_W:(hd + 1) * HEAD_W]
        s = lax.dot_general(qh, kh, (((1,), (1,)), ((), ())), preferred_element_type=F32)
        s = s + bias_scr[...]
        m_prev = m_scr[hd]
        m_new = jnp.maximum(m_prev, jnp.max(s, axis=1, keepdims=True))
        alpha = jnp.exp(m_prev - m_new)
        pr = jnp.exp(s - jnp.concatenate([m_new] * reps, axis=1))
        l_scr[hd] = alpha * l_scr[hd] + jnp.sum(pr, axis=1, keepdims=True)
        m_scr[hd] = m_new
        pv = jnp.dot(pr.astype(BF16), v_ref[0, :, hd * V_DIM:(hd + 1) * V_DIM],
                     preferred_element_type=F32)
        cs = slice(hd * V_DIM, (hd + 1) * V_DIM)
        acc_scr[:, cs] = acc_scr[:, cs] * alpha + pv

    @pl.when(kj == qi)
    def _():
        for hd in range(N_HEADS):
            cs = slice(hd * V_DIM, (hd + 1) * V_DIM)
            o_ref[0, :, cs] = (acc_scr[:, cs] / l_scr[hd]).astype(o_ref.dtype)


def _prompt_attention(q, k, v):
    b, s, _ = q.shape
    tq = _row_tile(s, 512)
    assert tq % CHUNK == 0 and V_DIM == LANES
    nq = s // tq
    pairs = [(i, j) for i in range(nq) for j in range(i + 1)]
    qi = jnp.asarray(np.array([p[0] for p in pairs], np.int32))
    kj = jnp.asarray(np.array([p[1] for p in pairs], np.int32))
    grid_spec = pltpu.PrefetchScalarGridSpec(
        num_scalar_prefetch=2,
        grid=(b, len(pairs)),
        in_specs=[pl.BlockSpec((1, tq, N_HEADS * HEAD_W), lambda bb, p, qi, kj: (bb, qi[p], 0)),
                  pl.BlockSpec((1, tq, N_HEADS * HEAD_W), lambda bb, p, qi, kj: (bb, kj[p], 0)),
                  pl.BlockSpec((1, tq, N_HEADS * V_DIM), lambda bb, p, qi, kj: (bb, kj[p], 0))],
        out_specs=pl.BlockSpec((1, tq, N_HEADS * V_DIM), lambda bb, p, qi, kj: (bb, qi[p], 0)),
        scratch_shapes=[pltpu.VMEM((N_HEADS, tq, LANES), F32),
                        pltpu.VMEM((N_HEADS, tq, LANES), F32),
                        pltpu.VMEM((tq, N_HEADS * V_DIM), F32),
                        pltpu.VMEM((tq, tq), F32)],
    )
    return pl.pallas_call(
        _flash_kernel,
        grid_spec=grid_spec,
        out_shape=jax.ShapeDtypeStruct((b, s, N_HEADS * V_DIM), BF16),
        compiler_params=_cparams(("parallel", "arbitrary")),
        name="prompt_attention",
    )(qi, kj, q, k, v)


def _sample_attn_kernel(q_ref, kp_ref, vp_ref, kn_ref, vn_ref, o_ref, *, past_len):
    sq = q_ref.shape[1]
    shift = CHUNK.bit_length() - 1
    qc_p = (lax.broadcasted_iota(jnp.int32, (sq, past_len), 0) + past_len) >> shift
    kc_p = lax.broadcasted_iota(jnp.int32, (sq, past_len), 1) >> shift
    qc_n = (lax.broadcasted_iota(jnp.int32, (sq, sq), 0) + past_len) >> shift
    kc_n = (lax.broadcasted_iota(jnp.int32, (sq, sq), 1) + past_len) >> shift
    dn = (((1,), (1,)), ((), ()))
    for hd in range(N_HEADS):
        hs = slice(hd * HEAD_W, (hd + 1) * HEAD_W)
        vs = slice(hd * V_DIM, (hd + 1) * V_DIM)
        qh = q_ref[0, :, hs]
        sp = lax.dot_general(qh, kp_ref[0, :, hs], dn, preferred_element_type=F32)
        sn = lax.dot_general(qh, kn_ref[0, :, hs], dn, preferred_element_type=F32)
        sp = jnp.where(kc_p <= qc_p, sp, NEG_BIG)
        sn = jnp.where(kc_n <= qc_n, sn, NEG_BIG)
        m = jnp.maximum(jnp.max(sp, axis=1, keepdims=True), jnp.max(sn, axis=1, keepdims=True))
        pp = jnp.exp(sp - m)
        pn = jnp.exp(sn - m)
        l = jnp.sum(pp, axis=1, keepdims=True) + jnp.sum(pn, axis=1, keepdims=True)
        o = (jnp.dot(pp.astype(BF16), vp_ref[0, :, vs], preferred_element_type=F32)
             + jnp.dot(pn.astype(BF16), vn_ref[0, :, vs], preferred_element_type=F32))
        o_ref[0, :, vs] = (o / l).astype(o_ref.dtype)


def _sample_attention(q, kp, vp, kn, vn):
    b, sq, _ = q.shape
    past_len = kp.shape[1]
    blk = lambda n, w: pl.BlockSpec((1, n, w), lambda i: (i, 0, 0))
    return pl.pallas_call(
        functools.partial(_sample_attn_kernel, past_len=past_len),
        grid=(b,),
        in_specs=[blk(sq, N_HEADS * HEAD_W), blk(past_len, N_HEADS * HEAD_W),
                  blk(past_len, N_HEADS * V_DIM), blk(sq, N_HEADS * HEAD_W), blk(sq, N_HEADS * V_DIM)],
        out_specs=blk(sq, N_HEADS * V_DIM),
        out_shape=jax.ShapeDtypeStruct((b, sq, N_HEADS * V_DIM), BF16),
        compiler_params=_cparams(("parallel",)),
        name="sample_attention",
    )(q, kp, vp, kn, vn)


HALO = 32


def _conv_kernel(u_ref, halo_ref, past_ref, w_ref, b_ref, lg_ref, lb_ref, o_ref,
                 win_scr, sh_scr, dw_scr, *, conv_w):
    i = pl.program_id(1)
    ts = u_ref.shape[1]
    ch = u_ref.shape[2]
    lead = HALO - (conv_w - 1)

    @pl.when(i == 0)
    def _():
        win_scr[0:HALO, :] = past_ref[0]

    @pl.when(i > 0)
    def _():
        win_scr[0:HALO, :] = halo_ref[0]

    win_scr[HALO:HALO + ts, :] = u_ref[0]
    span = ts + HALO - SUBLANES
    for b in range(1, SUBLANES):
        sh_scr[b - 1, :, :] = win_scr[b:b + span, :]

    def rows(rb, carry):
        r0 = pl.multiple_of(rb * SUBLANES, SUBLANES)
        acc = jnp.zeros((SUBLANES, ch), F32)
        for j in range(conv_w):
            off = lead + j
            rsl = pl.ds(r0 + (off // SUBLANES) * SUBLANES, SUBLANES)
            tap = win_scr[rsl, :] if off % SUBLANES == 0 else sh_scr[off % SUBLANES - 1, rsl, :]
            acc = acc + w_ref[j:j + 1, :] * tap
        dw_scr[pl.ds(r0, SUBLANES), :] = acc + b_ref[...]
        return carry

    lax.fori_loop(0, ts // SUBLANES, rows, 0)
    dw = dw_scr[...]
    mu = jnp.mean(dw, axis=-1, keepdims=True)
    xc = dw - mu
    var = jnp.mean(xc * xc, axis=-1, keepdims=True)
    y = xc * lax.rsqrt(var + LN_EPS) * lg_ref[...] + lb_ref[...]
    o_ref[0] = (y * _sigmoid(y)).astype(o_ref.dtype)


def _conv_block(u, past, w, b, lg, lb):
    bsz, s, ch = u.shape
    conv_w = w.shape[0]
    assert conv_w - 1 <= HALO and s % HALO == 0
    ts = _row_tile(s, 256)
    hb = ts // HALO
    return pl.pallas_call(
        functools.partial(_conv_kernel, conv_w=conv_w),
        grid=(bsz, s // ts),
        in_specs=[pl.BlockSpec((1, ts, ch), lambda bb, i: (bb, i, 0)),
                  pl.BlockSpec((1, HALO, ch), lambda bb, i: (bb, jnp.maximum(i * hb - 1, 0), 0)),
                  pl.BlockSpec((1, HALO, ch), lambda bb, i: (bb, 0, 0)),
                  _const_spec(w.shape), _const_spec(b.shape), _const_spec(lg.shape),
                  _const_spec(lb.shape)],
        out_specs=pl.BlockSpec((1, ts, ch), lambda bb, i: (bb, i, 0)),
        out_shape=jax.ShapeDtypeStruct((bsz, s, ch), BF16),
        scratch_shapes=[pltpu.VMEM((HALO + ts, ch), F32),
                        pltpu.VMEM((SUBLANES - 1, HALO + ts - SUBLANES, ch), F32),
                        pltpu.VMEM((ts, ch), F32)],
        compiler_params=_cparams(("parallel", "arbitrary")),
        name="conv_block",
    )(u, u, past, w, b, lg, lb)


def _outproj_kernel(attn_ref, cv_ref, x_ref, wa_ref, wc_ref, g2_ref, wrh_ref, wrl_ref, br_ref,
                    x1_ref, h2_ref, ew_ref, ei_ref):
    o = (jnp.dot(attn_ref[...], wa_ref[...], preferred_element_type=F32)
         + jnp.dot(cv_ref[...], wc_ref[...], preferred_element_type=F32))
    x1 = x_ref[...] + o
    x1_ref[...] = x1
    h2 = _rms(x1, g2_ref[...])
    h2_ref[...] = h2
    hi = h2.astype(BF16)
    lo = (h2 - hi.astype(F32)).astype(BF16)
    logits = (jnp.dot(hi, wrh_ref[...], preferred_element_type=F32)
              + jnp.dot(lo, wrh_ref[...], preferred_element_type=F32)
              + jnp.dot(hi, wrl_ref[...], preferred_element_type=F32)) + br_ref[...]
    lane = lax.broadcasted_iota(jnp.int32, logits.shape, 1)
    big = jnp.int32(LANES)
    is_g = (lane >= N_EXPERTS) & (lane < N_EXPERTS + N_GROUPS)
    lg = jnp.where(is_g, logits, -jnp.inf)
    mg = jnp.max(lg, axis=1, keepdims=True)
    sg = jnp.sum(jnp.exp(lg - mg), axis=1, keepdims=True)
    gv = 1.0 / sg
    gi = jnp.min(jnp.where(lg == mg, lane, big), axis=1, keepdims=True) - N_EXPERTS
    eshift = EXPERTS_PER_GROUP.bit_length() - 1
    in_grp = ((lane >> eshift) == gi) & (lane < N_EXPERTS)
    le = jnp.where(in_grp, logits, -jnp.inf)
    me = jnp.max(le, axis=1, keepdims=True)
    pe = jnp.exp(le - me)
    pe = pe / jnp.sum(pe, axis=1, keepdims=True)
    v1 = jnp.max(pe, axis=1, keepdims=True)
    i1 = jnp.min(jnp.where(in_grp & (pe == v1), lane, big), axis=1, keepdims=True)
    rest = in_grp & (lane != i1)
    pe2 = jnp.where(rest, pe, -1.0)
    v2 = jnp.max(pe2, axis=1, keepdims=True)
    i2 = jnp.min(jnp.where(rest & (pe2 == v2), lane, big), axis=1, keepdims=True)
    den = v1 + v2
    w1 = v1 / den * gv
    w2 = v2 / den * gv
    ew_ref[...] = jnp.where(lane == 0, w1, jnp.where(lane == 1, w2, 0.0))
    ei_ref[...] = jnp.where(lane == 0, i1, jnp.where(lane == 1, i2, 0))


def _outproj(attn, cv, x, wa, wc, g2, wrh, wrl, br):
    t, d = x.shape
    tm = _row_tile(t, 256)
    row = lambda w: pl.BlockSpec((tm, w), lambda i: (i, 0))
    return pl.pallas_call(
        _outproj_kernel,
        grid=(t // tm,),
        in_specs=[row(attn.shape[1]), row(cv.shape[1]), row(d), _const_spec(wa.shape),
                  _const_spec(wc.shape), _const_spec(g2.shape), _const_spec(wrh.shape),
                  _const_spec(wrl.shape), _const_spec(br.shape)],
        out_specs=[row(d), row(d), row(LANES), row(LANES)],
        out_shape=[jax.ShapeDtypeStruct((t, d), F32), jax.ShapeDtypeStruct((t, d), F32),
                   jax.ShapeDtypeStruct((t, LANES), F32), jax.ShapeDtypeStruct((t, LANES), jnp.int32)],
        compiler_params=_cparams(("parallel",)),
        name="outproj_router",
    )(attn, cv, x, wa, wc, g2, wrh, wrl, br)


EXPERT_ROWS = 256


def _moe_kernel(be_ref, nv_ref, tok_ref, asg_ref, h_hbm, wg_ref, wu_ref, wd_ref, y_hbm,
                xbuf, ybuf, wgb, wub, wdb, gsem, ssem):
    i = pl.program_id(0)
    nv = nv_ref[i]

    @pl.when(i == 0)
    def _():
        xbuf[...] = jnp.zeros(xbuf.shape, F32)

    def gather_copy(r, tok):
        return pltpu.make_async_copy(h_hbm.at[pl.ds(tok, 1), :], xbuf.at[pl.ds(r, 1), :], gsem)

    def scatter_copy(r, dst):
        return pltpu.make_async_copy(ybuf.at[pl.ds(r, 1), :], y_hbm.at[pl.ds(dst, 1), :], ssem)

    @pl.when(nv > 0)
    def _():
        def g_start(r, c):
            gather_copy(r, tok_ref[0, 0, r]).start()
            return c

        lax.fori_loop(0, nv, g_start, 0)

        prev = be_ref[jnp.maximum(i - 1, 0)]

        @pl.when((i == 0) | (be_ref[i] != prev))
        def _():
            wgb[...] = wg_ref[0].astype(BF16)
            wub[...] = wu_ref[0].astype(BF16)
            wdb[...] = wd_ref[0].astype(BF16)

        def g_wait(r, c):
            gather_copy(r, 0).wait()
            return c

        lax.fori_loop(0, nv, g_wait, 0)

        xb = xbuf[...].astype(BF16)
        g = jnp.dot(xb, wgb[...], preferred_element_type=F32)
        u = jnp.dot(xb, wub[...], preferred_element_type=F32)
        act = (g * _sigmoid(g)) * u
        ybuf[...] = jnp.dot(act.astype(BF16), wdb[...], preferred_element_type=F32)

        def s_start(r, c):
            scatter_copy(r, asg_ref[0, 0, r]).start()
            return c

        lax.fori_loop(0, nv, s_start, 0)

        def s_wait(r, c):
            scatter_copy(r, 0).wait()
            return c

        lax.fori_loop(0, nv, s_wait, 0)


def _dispatch(ei):
    t = ei.shape[0]
    a = t * TOP_K
    bm = EXPERT_ROWS
    nbm = -(-a // bm) + N_EXPERTS
    flat_e = ei[:, :TOP_K].reshape(-1)
    order = jnp.argsort(flat_e).astype(jnp.int32)
    counts = jnp.bincount(flat_e, length=N_EXPERTS).astype(jnp.int32)
    start = jnp.cumsum(counts) - counts
    padded = ((counts + bm - 1) // bm) * bm
    pend = jnp.cumsum(padded)
    pstart = pend - padded
    blk = jnp.arange(nbm, dtype=jnp.int32)
    block_e = jnp.minimum(jnp.searchsorted(pend, blk * bm, side='right'), N_EXPERTS - 1).astype(jnp.int32)
    j0 = blk * bm - pstart[block_e]
    nvalid = jnp.clip(counts[block_e] - j0, 0, bm).astype(jnp.int32)
    src = start[block_e][:, None] + j0[:, None] + jnp.arange(bm, dtype=jnp.int32)[None, :]
    asg = order[jnp.clip(src, 0, a - 1)]
    tok = asg // TOP_K
    return block_e, nvalid, tok.reshape(nbm, 1, bm), asg.reshape(nbm, 1, bm)


def _experts(h2, ei, w_gate, w_up, w_down):
    t, d = h2.shape
    de = w_gate.shape[2]
    bm = EXPERT_ROWS
    block_e, nvalid, tok, asg = _dispatch(ei)
    nbm = block_e.shape[0]
    idx_spec = pl.BlockSpec((1, 1, bm), lambda i, be, nv: (i, 0, 0), memory_space=pltpu.SMEM)
    grid_spec = pltpu.PrefetchScalarGridSpec(
        num_scalar_prefetch=2,
        grid=(nbm,),
        in_specs=[idx_spec, idx_spec,
                  pl.BlockSpec(memory_space=pl.ANY),
                  pl.BlockSpec((1, d, de), lambda i, be, nv: (be[i], 0, 0)),
                  pl.BlockSpec((1, d, de), lambda i, be, nv: (be[i], 0, 0)),
                  pl.BlockSpec((1, de, d), lambda i, be, nv: (be[i], 0, 0))],
        out_specs=pl.BlockSpec(memory_space=pl.ANY),
        scratch_shapes=[pltpu.VMEM((bm, d), F32), pltpu.VMEM((bm, d), F32),
                        pltpu.VMEM((d, de), BF16), pltpu.VMEM((d, de), BF16), pltpu.VMEM((de, d), BF16),
                        pltpu.SemaphoreType.DMA, pltpu.SemaphoreType.DMA],
    )
    return pl.pallas_call(
        _moe_kernel,
        grid_spec=grid_spec,
        out_shape=jax.ShapeDtypeStruct((t * TOP_K, d), F32),
        compiler_params=_cparams(("arbitrary",)),
        name="experts",
    )(block_e, nvalid, tok, asg, h2, w_gate, w_up, w_down)


def _combine_kernel(x1_ref, y_ref, ew_ref, x2_ref):
    d = x1_ref.shape[1]
    ew = ew_ref[...]
    x2_ref[...] = x1_ref[...] + (y_ref[:, :d] * ew[:, 0:1] + y_ref[:, d:] * ew[:, 1:2])


def _combine_norm_kernel(x1_ref, y_ref, ew_ref, g_ref, out_ref):
    d = x1_ref.shape[1]
    ew = ew_ref[...]
    x2 = x1_ref[...] + (y_ref[:, :d] * ew[:, 0:1] + y_ref[:, d:] * ew[:, 1:2])
    out_ref[...] = _rms(x2, g_ref[...])


def _combine(x1, ysc, ew, final_g=None):
    t, d = x1.shape
    tm = _row_tile(t, 256)
    row = lambda w: pl.BlockSpec((tm, w), lambda i: (i, 0))
    y2 = ysc.reshape(t, TOP_K * d)
    in_specs = [row(d), row(TOP_K * d), row(LANES)]
    args = [x1, y2, ew]
    kern = _combine_kernel
    if final_g is not None:
        in_specs.append(_const_spec(final_g.shape))
        args.append(final_g)
        kern = _combine_norm_kernel
    return pl.pallas_call(
        kern,
        grid=(t // tm,),
        in_specs=in_specs,
        out_specs=row(d),
        out_shape=jax.ShapeDtypeStruct((t, d), F32),
        compiler_params=_cparams(("parallel",)),
        name="combine",
    )(*args)


def _rot_cols(w):
    half = ROPE_DIM // 2
    return jnp.concatenate([-w[:, half:], w[:, :half]], axis=1)


def _pad_cols(w, width):
    return jnp.pad(w, ((0, 0), (0, width - w.shape[1])))


def _prep_w_in(w_in, q_lora, kv_lora, conv_ch):
    c0 = q_lora + kv_lora
    kr = w_in[:, c0:c0 + ROPE_DIM]
    conv = w_in[:, c0 + ROPE_DIM:c0 + ROPE_DIM + 2 * conv_ch]
    return jnp.concatenate([w_in[:, :c0], conv, _pad_cols(kr, ROPE_PAD),
                            _pad_cols(_rot_cols(kr), ROPE_PAD)], axis=1).astype(BF16)


def _prep_w_uq(w_uq):
    hw = NOPE_DIM + ROPE_DIM
    main, rot = [], []
    for hd in range(N_HEADS):
        nope = w_uq[:, hd * hw:hd * hw + NOPE_DIM]
        rope = w_uq[:, hd * hw + NOPE_DIM:(hd + 1) * hw]
        main.append(_pad_cols(jnp.concatenate([nope, rope], axis=1), HEAD_W))
        rot.append(_pad_cols(_rot_cols(rope), ROPE_PAD))
    return jnp.concatenate(main + rot, axis=1).astype(BF16)


def _prep_w_ukv(w_ukv):
    hw = NOPE_DIM + V_DIM
    ks = [w_ukv[:, hd * hw:hd * hw + NOPE_DIM] for hd in range(N_HEADS)]
    vs = [w_ukv[:, hd * hw + NOPE_DIM:(hd + 1) * hw] for hd in range(N_HEADS)]
    return jnp.concatenate(ks + vs, axis=1).astype(BF16)


def _prep_router(w_rg, b_rg, w_re, b_re):
    w = _pad_cols(jnp.concatenate([w_re, w_rg], axis=1), LANES)
    hi = w.astype(BF16)
    lo = (w - hi.astype(F32)).astype(BF16)
    b = _pad_cols(jnp.concatenate([b_re, b_rg])[None, :], LANES)
    return hi, lo, b


def _rope_tables(pos):
    half = ROPE_DIM // 2
    inv = ROPE_THETA ** (-jnp.arange(half, dtype=F32) / half)
    ang = pos.astype(F32)[:, None] * inv[None, :]
    cos, sin = jnp.cos(ang), jnp.sin(ang)
    return (_pad_cols(jnp.concatenate([cos, cos], axis=1), ROPE_PAD),
            _pad_cols(jnp.concatenate([sin, sin], axis=1), ROPE_PAD))


def kernel(x_prompt, x_sample, cache_kv, cache_krope, state_conv, norm1_g, w_in, q_norm_g, w_uq, kv_norm_g, w_ukv, conv_w, conv_b, conv_ln_g, conv_ln_b, w_out, norm2_g, w_router_group, b_router_group, w_router_expert, b_router_expert, w_gate, w_up, w_down, final_norm_g):
    bp, sp, d = x_prompt.shape
    bs, ss, _ = x_sample.shape
    depth = w_in.shape[0]
    past_len = cache_kv.shape[2]
    kv_lora = cache_kv.shape[3]
    q_lora = q_norm_g.shape[1]
    conv_ch = conv_w.shape[2]
    hist = conv_w.shape[1] - 1
    tp, ts_ = bp * sp, bs * ss
    attn_w = N_HEADS * V_DIM

    x = jnp.concatenate([x_prompt.reshape(tp, d), x_sample.reshape(ts_, d)], axis=0)
    pos = jnp.concatenate([jnp.tile(jnp.arange(sp), bp), jnp.tile(past_len + jnp.arange(ss), bs)])
    cos, sin = _rope_tables(pos)
    zero_hist = jnp.zeros((bp, HALO, conv_ch), F32)

    kv_p, kr_p, cv_p, kv_s, kr_s, cv_s = [], [], [], [], [], []
    y = None
    for l in range(depth):
        win = _prep_w_in(w_in[l], q_lora, kv_lora, conv_ch)
        wuq = _prep_w_uq(w_uq[l])
        wukv = _prep_w_ukv(w_ukv[l])
        q, ckv, kr, kf, vf, u = _inproj(x, norm1_g[l][None], win, q_norm_g[l][None], wuq,
                                        kv_norm_g[l][None], wukv, cos, sin, conv_ch=conv_ch)
        attn_p = _prompt_attention(q[:tp].reshape(bp, sp, -1), kf[:tp].reshape(bp, sp, -1),
                                   vf[:tp].reshape(bp, sp, -1))
        kpast, vpast = _expand_cache(cache_kv[l].reshape(bs * past_len, kv_lora),
                                     cache_krope[l].reshape(bs * past_len, ROPE_DIM), wukv)
        attn_s = _sample_attention(q[tp:].reshape(bs, ss, -1), kpast.reshape(bs, past_len, -1),
                                   vpast.reshape(bs, past_len, -1), kf[tp:].reshape(bs, ss, -1),
                                   vf[tp:].reshape(bs, ss, -1))
        attn = jnp.concatenate([attn_p.reshape(tp, attn_w), attn_s.reshape(ts_, attn_w)], axis=0)
        u_p = u[:tp].reshape(bp, sp, conv_ch)
        u_s = u[tp:].reshape(bs, ss, conv_ch)
        hist_s = jnp.pad(state_conv[l], ((0, 0), (HALO - hist, 0), (0, 0)))
        cw = (conv_w[l], conv_b[l][None], conv_ln_g[l][None], conv_ln_b[l][None])
        cv = jnp.concatenate([_conv_block(u_p, zero_hist, *cw).reshape(tp, conv_ch),
                              _conv_block(u_s, hist_s, *cw).reshape(ts_, conv_ch)], axis=0)
        wo = w_out[l].astype(BF16)
        wrh, wrl, br = _prep_router(w_router_group[l], b_router_group[l], w_router_expert[l],
                                    b_router_expert[l])
        x1, h2, ew, ei = _outproj(attn, cv, x, wo[:attn_w], wo[attn_w:], norm2_g[l][None], wrh, wrl, br)
        ysc = _experts(h2, ei, w_gate[l], w_up[l], w_down[l])
        if l + 1 < depth:
            x = _combine(x1, ysc, ew)
        else:
            y = _combine(x1, ysc, ew, final_norm_g[None])
        kv_p.append(ckv[:tp].reshape(bp, sp, kv_lora))
        kr_p.append(kr[:tp].reshape(bp, sp, ROPE_DIM))
        cv_p.append(u_p[:, sp - hist:])
        kv_s.append(ckv[tp:].reshape(bs, ss, kv_lora))
        kr_s.append(kr[tp:].reshape(bs, ss, ROPE_DIM))
        cv_s.append(jnp.concatenate([state_conv[l], u_s], axis=1)[:, -hist:])
    return (y[:tp].reshape(bp, sp, d), y[tp:].reshape(bs, ss, d), jnp.stack(kv_p), jnp.stack(kr_p),
            jnp.stack(cv_p), jnp.stack(kv_s), jnp.stack(kr_s), jnp.stack(cv_s))
```
